```python
import math
import jax, jax.numpy as jnp
from jax import lax
import numpy as np

D_MODEL = 1024
BATCH = 8
SEQ = 2048
DEPTH = 2
DEC_BATCH = 128
DEC_SEQ = 1
PAST_LEN = 2048
PAGE_SIZE = 128

HEAD_DIM = 64
MIX_WIDTH = D_MODEL
A_HEADS = (MIX_WIDTH // 2) // HEAD_DIM
A_WIDTH = A_HEADS * HEAD_DIM
B_WIDTH = MIX_WIDTH - A_WIDTH
B_GROUPS = 4
B_GROUP_DIM = B_WIDTH // B_GROUPS
CHUNK = 128
MOBA_BLOCK = 256
MOBA_TOPK = 3
ROPE_THETA = 500000.0
ROPE_DIMS = HEAD_DIM // 4
C_HEADS = (MIX_WIDTH // 2) // HEAD_DIM
C_WIDTH = C_HEADS * HEAD_DIM
D_WIDTH = MIX_WIDTH - C_WIDTH
POOL_WINDOWS = (2, 4, 8, 16)
POOL_GROUPS = len(POOL_WINDOWS)
POOL_GROUP_DIM = D_WIDTH // POOL_GROUPS
POOL_STATE = max(POOL_WINDOWS) - 1
Q_BLOCK = 128
D_FF = ((8 * D_MODEL // 3 + 255) // 256) * 256
CONV_W = 3
EPS = 1e-6

kernel_name = "moba_gmlp_stickbreaking_pool_convffn_step"


def rmsnorm(x, g):
    xf = x.astype(jnp.float32)
    y = xf * lax.rsqrt(jnp.mean(xf * xf, axis=-1, keepdims=True) + EPS)
    return (y * g).astype(x.dtype)


def layer_norm(x, g):
    xf = x.astype(jnp.float32)
    mu = jnp.mean(xf, axis=-1, keepdims=True)
    xc = xf - mu
    y = xc * lax.rsqrt(jnp.mean(xc * xc, axis=-1, keepdims=True) + EPS)
    return (y * g).astype(x.dtype)


def rope_partial(x, pos):
    half = ROPE_DIMS // 2
    inv_freq = ROPE_THETA ** (-(jnp.arange(half, dtype=jnp.float32) * 2.0 / ROPE_DIMS))
    ang = pos.astype(jnp.float32)[:, None] * inv_freq[None, :]
    cos = jnp.cos(ang)[None, :, None, :]
    sin = jnp.sin(ang)[None, :, None, :]
    xf = x.astype(jnp.float32)
    x1, x2, rest = xf[..., :half], xf[..., half:ROPE_DIMS], xf[..., ROPE_DIMS:]
    out = jnp.concatenate([x1 * cos - x2 * sin, x2 * cos + x1 * sin, rest], axis=-1)
    return out.astype(x.dtype)


def gather_pages(cache, page_table):
    g = cache[page_table]
    db, npg, pg = g.shape[0], g.shape[1], g.shape[2]
    return g.reshape(db, npg * pg, *g.shape[3:])


def moba_attention(q, k, v, q_pos):
    B, Tq, H, dh = q.shape
    L = k.shape[1]
    nb = -(-L // MOBA_BLOCK)
    pad = nb * MOBA_BLOCK - L
    kb = jnp.pad(k, ((0, 0), (0, pad), (0, 0), (0, 0))).reshape(B, nb, MOBA_BLOCK, H, dh)
    vb = jnp.pad(v, ((0, 0), (0, pad), (0, 0), (0, 0))).reshape(B, nb, MOBA_BLOCK, H, dh)
    kmean = jnp.mean(kb.astype(jnp.float32), axis=2)
    topk = min(MOBA_TOPK, nb)
    qb = Q_BLOCK if Tq % Q_BLOCK == 0 else Tq
    nqb = Tq // qb
    qs = q.reshape(B, nqb, qb, H, dh).transpose(0, 1, 3, 2, 4).reshape(B * nqb, H, qb, dh)
    ps = q_pos.reshape(nqb, qb)
    bidx = jnp.repeat(jnp.arange(B, dtype=jnp.int32), nqb)
    jidx = jnp.tile(jnp.arange(nqb, dtype=jnp.int32), B)
    scale = HEAD_DIM ** -0.5
    blk_off = jnp.arange(MOBA_BLOCK, dtype=jnp.int32)

    def one(args):
        qh, bi, ji = args
        pos = ps[ji]
        kbh = kb[bi].transpose(2, 0, 1, 3)
        vbh = vb[bi].transpose(2, 0, 1, 3)
        own = pos // MOBA_BLOCK
        gate = jnp.einsum('hqd,nhd->hqn', qh.astype(jnp.float32), kmean[bi])
        past = jnp.arange(nb)[None, None, :] < own[None, :, None]
        gate = jnp.where(past, gate, -jnp.inf)
        _, top = lax.top_k(gate, topk)
        top_ok = jnp.arange(topk)[None, None, :] < own[None, :, None]
        sel = jnp.concatenate([top, jnp.broadcast_to(own[None, :, None], (H, qb, 1))], axis=-1)
        ok = jnp.concatenate([jnp.broadcast_to(top_ok, (H, qb, topk)), jnp.ones((H, qb, 1), bool)], axis=-1)
        ks = jax.vmap(lambda kh, ih: kh[ih])(kbh, sel)
        vs = jax.vmap(lambda vh, ih: vh[ih])(vbh, sel)
        kpos = sel[..., None] * MOBA_BLOCK + blk_off
        mask = ok[..., None] & (kpos <= pos[None, :, None, None])
        s = jnp.einsum('hqd,hqjsd->hqjs', qh, ks).astype(jnp.float32) * scale
        s = jnp.where(mask, s, -jnp.inf)
        p = jax.nn.softmax(s.reshape(H, qb, -1), axis=-1).reshape(s.shape)
        return jnp.einsum('hqjs,hqjsd->qhd', p.astype(vs.dtype), vs)

    out = lax.map(one, (qs, bidx, jidx))
    return out.reshape(B, Tq, H, dh)


def stick_breaking_attention(q, k, v, q_pos):
    B, Tq, H, dh = q.shape
    L = k.shape[1]
    qb = Q_BLOCK if Tq % Q_BLOCK == 0 else Tq
    nqb = Tq // qb
    qs = q.reshape(B, nqb, qb, H, dh).transpose(1, 0, 2, 3, 4)
    ps = q_pos.reshape(nqb, qb)
    kpos = jnp.arange(L, dtype=jnp.int32)
    scale = HEAD_DIM ** -0.5

    def one(args):
        qh, pos = args
        z = jnp.einsum('bqhd,bkhd->bhqk', qh, k).astype(jnp.float32) * scale
        before = kpos[None, :] < pos[:, None]
        log_keep = jnp.where(before, jax.nn.log_sigmoid(-z), 0.0)
        later = lax.cumsum(log_keep, axis=3, reverse=True) - log_keep
        w = jnp.where(before, jnp.exp(jax.nn.log_sigmoid(z) + later), 0.0)
        return jnp.einsum('bhqk,bkhd->bqhd', w.astype(v.dtype), v)

    out = lax.map(one, (qs, ps))
    return out.transpose(1, 0, 2, 3, 4).reshape(B, Tq, H, dh)


def spatial_gating(u, v, sgu_w, sgu_b):
    B, T, _ = v.shape
    c = min(T, CHUNK)
    nc = T // c
    w = jnp.tril(sgu_w[:, :c, :c])
    vg = v.reshape(B, nc, c, B_GROUPS, B_GROUP_DIM)
    f = jnp.einsum('gts,bnsgd->bntgd', w, vg) + sgu_b[:, :c].T[None, None, :, :, None]
    return u * f.reshape(B, T, B_WIDTH)


def pool_mixer(u_prev, u_new, pos, pool_w, pool_scale):
    B, T, _ = u_new.shape
    P = u_prev.shape[1]
    u_all = jnp.concatenate([u_prev, u_new], axis=1).astype(jnp.float32)
    cs = jnp.pad(jnp.cumsum(u_all, axis=1), ((0, 0), (1, 0), (0, 0)))
    end = cs[:, P + 1:P + 1 + T]
    parts = []
    for g, w in enumerate(POOL_WINDOWS):
        sl = slice(g * POOL_GROUP_DIM, (g + 1) * POOL_GROUP_DIM)
        start = cs[:, P + 1 - w:P + 1 - w + T, sl]
        cnt = jnp.minimum(pos + 1, w).astype(jnp.float32)[None, :, None]
        parts.append((end[..., sl] - start) / cnt)
    pooled = jnp.concatenate(parts, axis=-1) - u_new.astype(jnp.float32)
    pooled = pooled.astype(u_new.dtype).reshape(B, T, POOL_GROUPS, POOL_GROUP_DIM)
    mixed = jnp.einsum('btgc,gcd->btgd', pooled, pool_w).reshape(B, T, D_WIDTH)
    return mixed * pool_scale


def moba_gmlp_mixer(x, pos, past_kv, norm_g, w_in, sgu_gain, sgu_w, sgu_b, w_out):
    B, T, _ = x.shape
    h = rmsnorm(x, norm_g)
    z = h @ w_in
    q, k, v, bu, bv = jnp.split(z, [A_WIDTH, 2 * A_WIDTH, 3 * A_WIDTH, 3 * A_WIDTH + B_WIDTH], axis=-1)
    q = rope_partial(q.reshape(B, T, A_HEADS, HEAD_DIM), pos)
    k = rope_partial(k.reshape(B, T, A_HEADS, HEAD_DIM), pos)
    v = v.reshape(B, T, A_HEADS, HEAD_DIM)
    new_kv = jnp.stack([k, v], axis=2)
    if past_kv is None:
        k_all, v_all = k, v
    else:
        k_all = jnp.concatenate([past_kv[:, :, 0], k], axis=1)
        v_all = jnp.concatenate([past_kv[:, :, 1], v], axis=1)
    a = moba_attention(q, k_all, v_all, pos).reshape(B, T, A_WIDTH)
    bvn = layer_norm(jax.nn.gelu(bv), sgu_gain)
    b_out = spatial_gating(jax.nn.gelu(bu), bvn, sgu_w, sgu_b)
    y = jnp.concatenate([a, b_out], axis=-1) @ w_out
    return x + y, new_kv, bvn


def sb_pool_mixer(x, pos, past_kv, pool_prev, norm_g, w_in, pool_w, pool_scale, w_out):
    B, T, _ = x.shape
    h = rmsnorm(x, norm_g)
    z = h @ w_in
    q, k, v, u = jnp.split(z, [C_WIDTH, 2 * C_WIDTH, 3 * C_WIDTH], axis=-1)
    q = q.reshape(B, T, C_HEADS, HEAD_DIM)
    k = k.reshape(B, T, C_HEADS, HEAD_DIM)
    v = v.reshape(B, T, C_HEADS, HEAD_DIM)
    new_kv = jnp.stack([k, v], axis=2)
    if past_kv is None:
        k_all, v_all = k, v
    else:
        k_all = jnp.concatenate([past_kv[:, :, 0], k], axis=1)
        v_all = jnp.concatenate([past_kv[:, :, 1], v], axis=1)
    c_out = stick_breaking_attention(q, k_all, v_all, pos).reshape(B, T, C_WIDTH)
    d_out = pool_mixer(pool_prev, u, pos, pool_w, pool_scale)
    new_pool = jnp.concatenate([pool_prev, u], axis=1)[:, -POOL_STATE:]
    y = jnp.concatenate([c_out, d_out], axis=-1) @ w_out
    return x + y, new_kv, new_pool


def conv_ffn(x, prev, norm_g, w_gate, w_up, conv_w, conv_b, w_down):
    T = x.shape[1]
    h = rmsnorm(x, norm_g)
    g = h @ w_gate
    g_all = jnp.concatenate([prev, g], axis=1)
    gc = sum(conv_w[i] * g_all[:, i:i + T] for i in range(CONV_W)) + conv_b
    a = jax.nn.gelu(gc) * (h @ w_up)
    return x + a @ w_down, g_all[:, -(CONV_W - 1):]


def setup_inputs(seed: int = 0) -> dict:
    key = jax.random.key(seed)
    ks = jax.random.split(key, 32)
    f32 = jnp.float32
    nrm = lambda k, shape, s: jax.random.normal(k, shape, f32) * s
    n_pages = PAST_LEN // PAGE_SIZE
    n_used = DEC_BATCH * n_pages
    n_phys = n_used + n_used // 4
    perm = jax.random.permutation(ks[0], n_phys)
    page_table = perm[:n_used].reshape(DEC_BATCH, n_pages).astype(jnp.int32)
    in0 = 3 * A_WIDTH + 2 * B_WIDTH
    in1 = 3 * C_WIDTH + D_WIDTH
    return {
        "x_prompt": nrm(ks[1], (BATCH, SEQ, D_MODEL), 1.0),
        "x_sample": nrm(ks[2], (DEC_BATCH, DEC_SEQ, D_MODEL), 1.0),
        "cache_l0_kv": nrm(ks[3], (n_phys, PAGE_SIZE, 2, A_HEADS, HEAD_DIM), 1.0),
        "cache_l1_kv": nrm(ks[4], (n_phys, PAGE_SIZE, 2, C_HEADS, HEAD_DIM), 1.0),
        "page_table": page_table,
        "state_l1_pool": nrm(ks[5], (DEC_BATCH, POOL_STATE, D_WIDTH), 1.0),
        "state_ffn_conv": nrm(ks[6], (DEPTH, DEC_BATCH, CONV_W - 1, D_FF), 1.0),
        "l0_norm": 1.0 + nrm(ks[7], (D_MODEL,), 0.02),
        "l0_w_in": nrm(ks[8], (D_MODEL, in0), D_MODEL ** -0.5),
        "l0_sgu_gain": 1.0 + nrm(ks[9], (B_WIDTH,), 0.02),
        "l0_sgu_w": nrm(ks[10], (B_GROUPS, CHUNK, CHUNK), CHUNK ** -0.5),
        "l0_sgu_b": 1.0 + nrm(ks[11], (B_GROUPS, CHUNK), 0.1),
        "l0_w_out": nrm(ks[12], (MIX_WIDTH, D_MODEL), MIX_WIDTH ** -0.5),
        "l1_norm": 1.0 + nrm(ks[13], (D_MODEL,), 0.02),
        "l1_w_in": nrm(ks[14], (D_MODEL, in1), D_MODEL ** -0.5),
        "l1_pool_w": nrm(ks[15], (POOL_GROUPS, POOL_GROUP_DIM, POOL_GROUP_DIM), POOL_GROUP_DIM ** -0.5),
        "l1_pool_scale": 1.0 + nrm(ks[16], (D_WIDTH,), 0.1),
        "l1_w_out": nrm(ks[17], (MIX_WIDTH, D_MODEL), MIX_WIDTH ** -0.5),
        "ffn_norm": 1.0 + nrm(ks[18], (DEPTH, D_MODEL), 0.02),
        "ffn_w_gate": nrm(ks[19], (DEPTH, D_MODEL, D_FF), D_MODEL ** -0.5),
        "ffn_w_up": nrm(ks[20], (DEPTH, D_MODEL, D_FF), D_MODEL ** -0.5),
        "ffn_conv_w": nrm(ks[21], (DEPTH, CONV_W, D_FF), CONV_W ** -0.5),
        "ffn_conv_b": nrm(ks[22], (DEPTH, D_FF), 0.02),
        "ffn_w_down": nrm(ks[23], (DEPTH, D_FF, D_MODEL), D_FF ** -0.5),
        "final_norm": 1.0 + nrm(ks[24], (D_MODEL,), 0.02),
    }


def reference(x_prompt, x_sample, cache_l0_kv, cache_l1_kv, page_table, state_l1_pool, state_ffn_conv,
              l0_norm, l0_w_in, l0_sgu_gain, l0_sgu_w, l0_sgu_b, l0_w_out,
              l1_norm, l1_w_in, l1_pool_w, l1_pool_scale, l1_w_out,
              ffn_norm, ffn_w_gate, ffn_w_up, ffn_conv_w, ffn_conv_b, ffn_w_down, final_norm):
    pos_p = jnp.arange(SEQ, dtype=jnp.int32)
    pos_s = PAST_LEN + jnp.arange(DEC_SEQ, dtype=jnp.int32)
    past0 = gather_pages(cache_l0_kv, page_table)
    past1 = gather_pages(cache_l1_kv, page_table)
    bp = x_prompt.shape[0]
    pool_zero = jnp.zeros((bp, POOL_STATE, D_WIDTH), x_prompt.dtype)
    conv_zero = jnp.zeros((bp, CONV_W - 1, D_FF), x_prompt.dtype)
    xp, xs = x_prompt, x_sample
    conv_p, conv_s = [], []
    for layer in range(DEPTH):
        if layer % 2 == 0:
            xp, kv0_p, _ = moba_gmlp_mixer(xp, pos_p, None, l0_norm, l0_w_in, l0_sgu_gain, l0_sgu_w, l0_sgu_b, l0_w_out)
            xs, kv0_s, sgu_v_s = moba_gmlp_mixer(xs, pos_s, past0, l0_norm, l0_w_in, l0_sgu_gain, l0_sgu_w, l0_sgu_b, l0_w_out)
        else:
            xp, kv1_p, pool_p = sb_pool_mixer(xp, pos_p, None, pool_zero, l1_norm, l1_w_in, l1_pool_w, l1_pool_scale, l1_w_out)
            xs, kv1_s, pool_s = sb_pool_mixer(xs, pos_s, past1, state_l1_pool, l1_norm, l1_w_in, l1_pool_w, l1_pool_scale, l1_w_out)
        xp, cp = conv_ffn(xp, conv_zero, ffn_norm[layer], ffn_w_gate[layer], ffn_w_up[layer], ffn_conv_w[layer], ffn_conv_b[layer], ffn_w_down[layer])
        xs, cs = conv_ffn(xs, state_ffn_conv[layer], ffn_norm[layer], ffn_w_gate[layer], ffn_w_up[layer], ffn_conv_w[layer], ffn_conv_b[layer], ffn_w_down[layer])
        conv_p.append(cp)
        conv_s.append(cs)
    y_prompt = rmsnorm(xp, final_norm)
    y_sample = rmsnorm(xs, final_norm)
    ffn_conv_p = jnp.stack(conv_p, axis=0)
    ffn_conv_s = jnp.stack(conv_s, axis=0)
    return (y_prompt, y_sample, kv0_p, kv0_s, sgu_v_s, kv1_p, kv1_s, pool_p, pool_s, ffn_conv_p, ffn_conv_s)
```

```python
import functools

import jax
import jax.numpy as jnp
from jax import lax
from jax.experimental import pallas as pl
from jax.experimental.pallas import tpu as pltpu

F32 = jnp.float32
BF16 = jnp.bfloat16

EPS = 1e-6
HEAD_DIM = 64
LANES = 128
HEADS_PER_LANE_BLOCK = LANES // HEAD_DIM
ROPE_THETA = 500000.0
ROPE_DIMS = HEAD_DIM // 4
ROPE_HALF = ROPE_DIMS // 2
MOBA_BLOCK = 256
MOBA_TOPK = 3
CHUNK = 128
POOL_WINDOWS = (2, 4, 8, 16)
POOL_CARRY = 16
CONV_W = 3
CONV_CARRY = 8
ATTN_SCALE = HEAD_DIM ** -0.5
NEG_INF = float("-inf")
VMEM_LIMIT = 56 * 1024 * 1024

_NT = (((1,), (1,)), ((), ()))


def _dot(a, b):
    return jnp.dot(a, b, preferred_element_type=F32)


def _dot_nt(a, b, precision=None):
    return lax.dot_general(a, b, _NT, preferred_element_type=F32, precision=precision)


def _rms(x, g):
    return x * lax.rsqrt(jnp.mean(x * x, axis=-1, keepdims=True) + EPS) * g


def _gelu(x):
    return jax.nn.gelu(x)


def _softplus(z):
    return jnp.maximum(z, 0.0) + jnp.log(1.0 + jnp.exp(-jnp.abs(z)))


def _rope(a, c, s_next, s_prev):
    return a * c + pltpu.roll(a, LANES - ROPE_HALF, 1) * s_next + pltpu.roll(a, ROPE_HALF, 1) * s_prev


def _rope_tables(pos):
    inv_freq = ROPE_THETA ** (-(jnp.arange(ROPE_HALF, dtype=F32) * 2.0 / ROPE_DIMS))
    ang = pos.astype(F32)[:, None] * inv_freq[None, :]
    cos, sin = jnp.cos(ang), jnp.sin(ang)
    t = pos.shape[0]
    rest = HEAD_DIM - ROPE_DIMS
    c = jnp.concatenate([cos, cos, jnp.ones((t, rest), F32)], axis=-1)
    s_next = jnp.concatenate([-sin, jnp.zeros((t, ROPE_HALF + rest), F32)], axis=-1)
    s_prev = jnp.concatenate([jnp.zeros((t, ROPE_HALF), F32), sin, jnp.zeros((t, rest), F32)], axis=-1)
    tile = lambda a: jnp.tile(a, (1, HEADS_PER_LANE_BLOCK))
    return tile(c), tile(s_next), tile(s_prev)


def _const_spec(shape):
    return pl.BlockSpec(shape, lambda *_: (0,) * len(shape), pipeline_mode=pl.Buffered(1))


def _params(n_axes):
    return pltpu.CompilerParams(dimension_semantics=("arbitrary",) * n_axes, vmem_limit_bytes=VMEM_LIMIT)


def _top3_select(gate, lane_idx, n_valid, n_blocks):
    valid = lane_idx < n_valid
    selm = jnp.zeros(gate.shape, F32)
    for n in range(n_blocks):
        gn = gate[:, n:n + 1]
        beats = ((gate > gn) | ((gate == gn) & (lane_idx < n))) & valid
        rank = jnp.sum(beats.astype(F32), axis=1, keepdims=True)
        selm = jnp.where(lane_idx == n, (rank < MOBA_TOPK).astype(F32), selm)
    return selm


def _gmlp_norm(zv, gain):
    gv = _gelu(zv)
    xc = gv - jnp.mean(gv, axis=-1, keepdims=True)
    return xc * lax.rsqrt(jnp.mean(xc * xc, axis=-1, keepdims=True) + EPS) * gain


def _inproj0_kernel(x_ref, g_ref, w_ref, c_ref, sn_ref, sp_ref, gain_ref, sw_ref, sbt_ref,
                    q_ref, kv_ref, bo_ref, km_ref, *, tm, width):
    h = _rms(x_ref[...], g_ref[...]).astype(BF16)
    z = _dot(h, w_ref[...])
    c, sn, sp = c_ref[...], sn_ref[...], sp_ref[...]
    for j in range(width // LANES):
        sl = slice(j * LANES, (j + 1) * LANES)
        q_ref[:, sl] = _rope(z[:, j * LANES:(j + 1) * LANES], c, sn, sp)
        kr = _rope(z[:, width + j * LANES:width + (j + 1) * LANES], c, sn, sp)
        kv_ref[:, sl] = kr
        for b in range(tm // MOBA_BLOCK):
            km_ref[b, :, sl] = jnp.mean(kr[b * MOBA_BLOCK:(b + 1) * MOBA_BLOCK], axis=0, keepdims=True)
    kv_ref[:, width:2 * width] = z[:, 2 * width:3 * width]
    gu = _gelu(z[:, 3 * width:4 * width])
    bvn = _gmlp_norm(z[:, 4 * width:5 * width], gain_ref[...]).astype(BF16)
    row = lax.broadcasted_iota(jnp.int32, (CHUNK, CHUNK), 0)
    col = lax.broadcasted_iota(jnp.int32, (CHUNK, CHUNK), 1)
    sbt = sbt_ref[...]
    for g in range(width // LANES):
        sl = slice(g * LANES, (g + 1) * LANES)
        wg = jnp.where(col <= row, sw_ref[g], 0.0).astype(BF16)
        for ci in range(tm // CHUNK):
            rs = slice(ci * CHUNK, (ci + 1) * CHUNK)
            f = _dot(wg, bvn[rs, sl]) + sbt[:, g:g + 1]
            bo_ref[rs, sl] = (gu[rs, sl] * f).astype(BF16)


def _inproj0(x, norm_g, w_in, tables, gain, sgu_w, sgu_b, seq_len, tm):
    n, d = x.shape
    width = w_in.shape[1] // 5
    tps = seq_len // tm
    row_spec = lambda w: pl.BlockSpec((tm, w), lambda i: (i, 0))
    tab_spec = pl.BlockSpec((tm, LANES), lambda i: (i % tps, 0))
    return pl.pallas_call(
        functools.partial(_inproj0_kernel, tm=tm, width=width),
        grid=(n // tm,),
        in_specs=[row_spec(d), _const_spec((1, d)), _const_spec(w_in.shape), tab_spec, tab_spec, tab_spec,
                  _const_spec((1, width)), _const_spec(sgu_w.shape), _const_spec((CHUNK, sgu_b.shape[0]))],
        out_specs=[row_spec(width), row_spec(2 * width), row_spec(width),
                   pl.BlockSpec((tm // MOBA_BLOCK, 1, width), lambda i: (i, 0, 0))],
        out_shape=[jax.ShapeDtypeStruct((n, width), F32), jax.ShapeDtypeStruct((n, 2 * width), F32),
                   jax.ShapeDtypeStruct((n, width), BF16), jax.ShapeDtypeStruct((n // MOBA_BLOCK, 1, width), F32)],
        compiler_params=_params(1),
        name="inproj0",
    )(x, norm_g.reshape(1, d), w_in, *tables, gain.reshape(1, width), sgu_w, sgu_b[:, :CHUNK].T)


def _inproj0_dec_kernel(x_ref, g_ref, w_ref, c_ref, sn_ref, sp_ref, gain_ref, sw_ref, sb_ref,
                        q_ref, kv_ref, bo_ref, bvn_ref, *, width):
    h = _rms(x_ref[...], g_ref[...]).astype(BF16)
    z = _dot(h, w_ref[...])
    c, sn, sp = c_ref[...], sn_ref[...], sp_ref[...]
    for j in range(width // LANES):
        sl = slice(j * LANES, (j + 1) * LANES)
        q_ref[:, sl] = _rope(z[:, j * LANES:(j + 1) * LANES], c, sn, sp)
        kv_ref[:, sl] = _rope(z[:, width + j * LANES:width + (j + 1) * LANES], c, sn, sp)
    kv_ref[:, width:2 * width] = z[:, 2 * width:3 * width]
    bvn = _gmlp_norm(z[:, 4 * width:5 * width], gain_ref[...])
    bvn_ref[...] = bvn
    bo_ref[...] = (_gelu(z[:, 3 * width:4 * width]) * (sw_ref[...] * bvn + sb_ref[...])).astype(BF16)


def _inproj0_dec(x, norm_g, w_in, tables, gain, sgu_w, sgu_b):
    n, d = x.shape
    width = w_in.shape[1] // 5
    lanes_per_group = width // sgu_w.shape[0]
    sw = jnp.repeat(sgu_w[:, 0, 0], lanes_per_group).reshape(1, width)
    sb = jnp.repeat(sgu_b[:, 0], lanes_per_group).reshape(1, width)
    return pl.pallas_call(
        functools.partial(_inproj0_dec_kernel, width=width),
        out_shape=[jax.ShapeDtypeStruct((n, width), F32), jax.ShapeDtypeStruct((n, 2 * width), F32),
                   jax.ShapeDtypeStruct((n, width), BF16), jax.ShapeDtypeStruct((n, width), F32)],
        compiler_params=pltpu.CompilerParams(vmem_limit_bytes=VMEM_LIMIT),
        name="inproj0_dec",
    )(x, norm_g.reshape(1, d), w_in, *tables, gain.reshape(1, width), sw, sb)


def _moba_kernel(q_ref, k_ref, v_ref, km_ref, o_ref, *, blk, n_blocks):
    qi = pl.program_id(2)
    q = q_ref[...]
    lane = lax.broadcasted_iota(jnp.int32, (1, LANES), 1)
    row = lax.broadcasted_iota(jnp.int32, (blk, blk), 0)
    col = lax.broadcasted_iota(jnp.int32, (blk, blk), 1)
    blk_idx = lax.broadcasted_iota(jnp.int32, (1, n_blocks), 1)
    km = km_ref[0]
    qbs, selms, carry = [], [], []
    d0 = pl.multiple_of(qi * blk, blk)
    kd = k_ref[pl.ds(d0, blk), :].astype(BF16)
    vd = v_ref[pl.ds(d0, blk), :].astype(BF16)
    for h in range(HEADS_PER_LANE_BLOCK):
        qh = jnp.where((lane // HEAD_DIM) == h, q, 0.0)
        gate = _dot_nt(qh, km, precision=lax.Precision.HIGHEST)
        selms.append(_top3_select(gate, blk_idx, qi, n_blocks))
        qb = (qh * ATTN_SCALE).astype(BF16)
        qbs.append(qb)
        s = jnp.where(col <= row, _dot_nt(qb, kd), NEG_INF)
        m = jnp.max(s, axis=1, keepdims=True)
        p = jnp.exp(s - m)
        carry += [m, jnp.sum(p, axis=1, keepdims=True), _dot(p.astype(BF16), vd)]

    def body(n, carry):
        r0 = pl.multiple_of(n * blk, blk)
        kn = k_ref[pl.ds(r0, blk), :].astype(BF16)
        vn = v_ref[pl.ds(r0, blk), :].astype(BF16)
        out = []
        for h in range(HEADS_PER_LANE_BLOCK):
            m, l, acc = carry[3 * h:3 * h + 3]
            sel = jnp.sum(jnp.where(blk_idx == n, selms[h], 0.0), axis=1, keepdims=True) > 0.5
            s = jnp.where(sel, _dot_nt(qbs[h], kn), NEG_INF)
            m_new = jnp.maximum(m, jnp.max(s, axis=1, keepdims=True))
            alpha = jnp.exp(m - m_new)
            p = jnp.exp(s - m_new)
            out += [m_new, alpha * l + jnp.sum(p, axis=1, keepdims=True), alpha * acc + _dot(p.astype(BF16), vn)]
        return tuple(out)

    carry = lax.fori_loop(0, qi, body, tuple(carry))
    outs = [carry[3 * h + 2] / carry[3 * h + 1] for h in range(HEADS_PER_LANE_BLOCK)]
    o_ref[...] = jnp.where(lane < HEAD_DIM, outs[0], outs[1]).astype(BF16)


def _moba_prefill(q, kv, km, batch, seq_len):
    n, width = q.shape
    blk = MOBA_BLOCK
    nq = seq_len // blk
    n_lb = width // LANES
    return pl.pallas_call(
        functools.partial(_moba_kernel, blk=blk, n_blocks=nq),
        grid=(batch, n_lb, nq),
        in_specs=[pl.BlockSpec((blk, LANES), lambda b, j, i: (b * nq + i, j)),
                  pl.BlockSpec((seq_len, LANES), lambda b, j, i: (b, j)),
                  pl.BlockSpec((seq_len, LANES), lambda b, j, i: (b, n_lb + j)),
                  pl.BlockSpec((1, nq, LANES), lambda b, j, i: (b, 0, j))],
        out_specs=pl.BlockSpec((blk, LANES), lambda b, j, i: (b * nq + i, j)),
        out_shape=jax.ShapeDtypeStruct((n, width), BF16),
        compiler_params=_params(3),
        name="moba_prefill",
    )(q, kv, kv, km.reshape(batch, nq, width))


def _strict_upper(n):
    r = lax.broadcasted_iota(jnp.int32, (n, n), 0)
    c = lax.broadcasted_iota(jnp.int32, (n, n), 1)
    return jnp.where(r > c, 1.0, 0.0).astype(BF16)


def _suffix_sums(lk, upper):
    hi = lk.astype(BF16)
    lo = (lk - hi.astype(F32)).astype(BF16)
    return _dot(hi, upper) + _dot(lo, upper)


def _sb_kernel(q_ref, k_ref, v_ref, o_ref, *, blk):
    qi = pl.program_id(2)
    q = q_ref[...]
    lane = lax.broadcasted_iota(jnp.int32, (1, LANES), 1)
    row = lax.broadcasted_iota(jnp.int32, (blk, blk), 0)
    col = lax.broadcasted_iota(jnp.int32, (blk, blk), 1)
    before = col < row
    upper = _strict_upper(blk)
    qbs = [(jnp.where((lane // HEAD_DIM) == h, q, 0.0) * ATTN_SCALE).astype(BF16) for h in range(HEADS_PER_LANE_BLOCK)]

    def block(z, vn, r, acc, mask):
        sp = _softplus(z)
        lk = -sp if mask is None else jnp.where(mask, -sp, 0.0)
        later_in = _suffix_sums(lk, upper)
        w = jnp.exp(z - sp + later_in + r)
        if mask is not None:
            w = jnp.where(mask, w, 0.0)
        return r + later_in[:, 0:1] + lk[:, 0:1], acc + _dot(w.astype(BF16), vn)

    d0 = pl.multiple_of(qi * blk, blk)
    kd = k_ref[pl.ds(d0, blk), :].astype(BF16)
    vd = v_ref[pl.ds(d0, blk), :].astype(BF16)
    carry = []
    for h in range(HEADS_PER_LANE_BLOCK):
        carry += list(block(_dot_nt(qbs[h], kd), vd, jnp.zeros((blk, 1), F32), jnp.zeros((blk, LANES), F32), before))

    def body(it, carry):
        n = qi - 1 - it
        r0 = pl.multiple_of(n * blk, blk)
        kn = k_ref[pl.ds(r0, blk), :].astype(BF16)
        vn = v_ref[pl.ds(r0, blk), :].astype(BF16)
        out = []
        for h in range(HEADS_PER_LANE_BLOCK):
            out += list(block(_dot_nt(qbs[h], kn), vn, carry[2 * h], carry[2 * h + 1], None))
        return tuple(out)

    carry = lax.fori_loop(0, qi, body, tuple(carry))
    o_ref[...] = jnp.where(lane < HEAD_DIM, carry[1], carry[3]).astype(BF16)


def _sb_prefill(q, kv, batch, seq_len):
    n, width = q.shape
    blk = MOBA_BLOCK
    nq = seq_len // blk
    n_lb = width // LANES
    return pl.pallas_call(
        functools.partial(_sb_kernel, blk=blk),
        grid=(batch, n_lb, nq),
        in_specs=[pl.BlockSpec((blk, LANES), lambda b, j, i: (b * nq + i, j)),
                  pl.BlockSpec((seq_len, LANES), lambda b, j, i: (b, j)),
                  pl.BlockSpec((seq_len, LANES), lambda b, j, i: (b, n_lb + j))],
        out_specs=pl.BlockSpec((blk, LANES), lambda b, j, i: (b * nq + i, j)),
        out_shape=jax.ShapeDtypeStruct((n, width), BF16),
        compiler_params=_params(3),
        name="sb_prefill",
    )(q, kv, kv)


def _head_rows(q, n_heads, width):
    hrow = lax.broadcasted_iota(jnp.int32, (n_heads, width), 0)
    lane = lax.broadcasted_iota(jnp.int32, (n_heads, width), 1)
    hmask = (lane // HEAD_DIM) == hrow
    return jnp.where(hmask, jnp.broadcast_to(q, (n_heads, width)), 0.0), hmask


def _moba_dec_kernel(pt_ref, *refs, n_pages, page, width):
    del pt_ref
    page_refs = refs[:n_pages]
    q_ref, kvn_ref, o_ref = refs[n_pages:]
    n_heads = width // HEAD_DIM
    pages_per_block = MOBA_BLOCK // page
    n_blocks = n_pages // pages_per_block
    qm, hmask = _head_rows(q_ref[0], n_heads, width)
    qb = (qm * ATTN_SCALE).astype(BF16)
    zs, ksums = [], []
    for p in range(n_pages):
        kp = page_refs[p][0, :, 0:width]
        ksums.append(jnp.sum(kp, axis=0, keepdims=True))
        zs.append(_dot_nt(qb, kp.astype(BF16)))
    z = jnp.concatenate(zs, axis=1)
    km = jnp.concatenate([sum(ksums[j * pages_per_block:(j + 1) * pages_per_block]) * (1.0 / MOBA_BLOCK)
                          for j in range(n_blocks)], axis=0)
    gate = _dot_nt(qm, km, precision=lax.Precision.HIGHEST)
    blk_idx = lax.broadcasted_iota(jnp.int32, (1, n_blocks), 1)
    selm = _top3_select(gate, blk_idx, n_blocks, n_blocks)
    mask = jnp.concatenate([jnp.broadcast_to(selm[:, n:n + 1], (n_heads, MOBA_BLOCK)) for n in range(n_blocks)], axis=1)
    s = jnp.where(mask > 0.5, z, NEG_INF)
    kvn = kvn_ref[0]
    kn = kvn[:, 0:width].astype(BF16).astype(F32)
    vn = kvn[:, width:2 * width].astype(BF16).astype(F32)
    s_self = jnp.sum(qb.astype(F32) * kn, axis=1, keepdims=True)
    m = jnp.maximum(jnp.max(s, axis=1, keepdims=True), s_self)
    p = jnp.exp(s - m)
    p_self = jnp.exp(s_self - m)
    l = jnp.sum(p, axis=1, keepdims=True) + p_self
    pb = p.astype(BF16)
    out = p_self.astype(BF16).astype(F32) * vn
    for pg in range(n_pages):
        out = out + _dot(pb[:, pg * page:(pg + 1) * page], page_refs[pg][0, :, width:2 * width].astype(BF16))
    out = out / l
    o_ref[0] = jnp.sum(jnp.where(hmask, out, 0.0), axis=0, keepdims=True)


def _sb_dec_kernel(pt_ref, *refs, n_pages, page, width):
    del pt_ref
    page_refs = refs[:n_pages]
    q_ref, o_ref = refs[n_pages:]
    n_heads = width // HEAD_DIM
    qm, hmask = _head_rows(q_ref[0], n_heads, width)
    qb = (qm * ATTN_SCALE).astype(BF16)
    z = jnp.concatenate([_dot_nt(qb, page_refs[p][0, :, 0:width].astype(BF16)) for p in range(n_pages)], axis=1)
    sp = _softplus(z)
    seg = MOBA_BLOCK
    upper = _strict_upper(seg)
    r = jnp.zeros((n_heads, 1), F32)
    later = [None] * (z.shape[1] // seg)
    for c in reversed(range(len(later))):
        lk = -sp[:, c * seg:(c + 1) * seg]
        lat = _suffix_sums(lk, upper)
        later[c] = lat + r
        r = r + lat[:, 0:1] + lk[:, 0:1]
    wb = jnp.exp(z - sp + jnp.concatenate(later, axis=1)).astype(BF16)
    out = jnp.zeros((n_heads, width), F32)
    for pg in range(n_pages):
        out = out + _dot(wb[:, pg * page:(pg + 1) * page], page_refs[pg][0, :, width:2 * width].astype(BF16))
    o_ref[0] = jnp.sum(jnp.where(hmask, out, 0.0), axis=0, keepdims=True)


def _paged_decode(kernel_fn, name, cache, page_table, q, kv_new):
    n_seq, n_pages = page_table.shape
    n_phys, page = cache.shape[:2]
    width = q.shape[1]
    cache2 = cache.reshape(n_phys, page, 2 * width)
    page_specs = [pl.BlockSpec((1, page, 2 * width), lambda b, pt, p=p: (pt[b, p], 0, 0)) for p in range(n_pages)]
    row_spec = lambda w: pl.BlockSpec((1, 1, w), lambda b, pt: (b, 0, 0))
    extra_specs, extra = [row_spec(width)], [q.reshape(n_seq, 1, width)]
    if kv_new is not None:
        extra_specs.append(row_spec(2 * width))
        extra.append(kv_new.reshape(n_seq, 1, 2 * width))
    out = pl.pallas_call(
        functools.partial(kernel_fn, n_pages=n_pages, page=page, width=width),
        grid_spec=pltpu.PrefetchScalarGridSpec(
            num_scalar_prefetch=1, grid=(n_seq,),
            in_specs=page_specs + extra_specs, out_specs=row_spec(width)),
        out_shape=jax.ShapeDtypeStruct((n_seq, 1, width), F32),
        compiler_params=_params(1),
        name=name,
    )(page_table, *([cache2] * n_pages), *extra)
    return out.reshape(n_seq, width)


def _mix_out(x_ref, a_ref, b_ref, wo_ref):
    half = a_ref.shape[1]
    return (x_ref[...] + _dot(a_ref[...].astype(BF16), wo_ref[0:half, :])
            + _dot(b_ref[...].astype(BF16), wo_ref[half:2 * half, :]))


def _ffn_kernel(x_ref, a_ref, b_ref, wo_ref, ng_ref, wg_ref, wu_ref, cw_ref, cb_ref, wd_ref, fn_ref,
                y_ref, st_ref, gs_ref, *, tm, tps, final):
    @pl.when(pl.program_id(0) % tps == 0)
    def _():
        gs_ref[0:CONV_CARRY, :] = jnp.zeros((CONV_CARRY, gs_ref.shape[1]), F32)

    x1 = _mix_out(x_ref, a_ref, b_ref, wo_ref)
    h = _rms(x1, ng_ref[...]).astype(BF16)
    g = _dot(h, wg_ref[...])
    gs_ref[CONV_CARRY:CONV_CARRY + tm, :] = g
    g1 = gs_ref[CONV_CARRY - 1:CONV_CARRY - 1 + tm, :]
    g2 = gs_ref[CONV_CARRY - 2:CONV_CARRY - 2 + tm, :]
    gc = cw_ref[0:1, :] * g2 + cw_ref[1:2, :] * g1 + cw_ref[2:3, :] * g + cb_ref[...]
    act = (_gelu(gc) * _dot(h, wu_ref[...])).astype(BF16)
    x2 = x1 + _dot(act, wd_ref[...])
    st_ref[0] = gs_ref[CONV_CARRY + tm - (CONV_W - 1):CONV_CARRY + tm, :]
    gs_ref[0:CONV_CARRY, :] = gs_ref[tm:tm + CONV_CARRY, :]
    y_ref[...] = _rms(x2, fn_ref[...]) if final else x2


def _ffn_dec_kernel(x_ref, a_ref, b_ref, wo_ref, ng_ref, wg_ref, wu_ref, cw_ref, cb_ref, wd_ref, fn_ref,
                    p0_ref, p1_ref, y_ref, g_ref, *, final):
    x1 = _mix_out(x_ref, a_ref, b_ref, wo_ref)
    h = _rms(x1, ng_ref[...]).astype(BF16)
    g = _dot(h, wg_ref[...])
    g_ref[...] = g
    gc = cw_ref[0:1, :] * p0_ref[...] + cw_ref[1:2, :] * p1_ref[...] + cw_ref[2:3, :] * g + cb_ref[...]
    act = (_gelu(gc) * _dot(h, wu_ref[...])).astype(BF16)
    x2 = x1 + _dot(act, wd_ref[...])
    y_ref[...] = _rms(x2, fn_ref[...]) if final else x2


def _ffn_weights(w_out, ng, wg, wu, cw, cb, wd, fn):
    d, f = wg.shape
    return (w_out, ng.reshape(1, d), wg, wu, cw, cb.reshape(1, f), wd, fn.reshape(1, d))


def _outproj_ffn(x, a, b, weights, seq_len, tm, final):
    n, d = x.shape
    f = weights[2].shape[1]
    tps = seq_len // tm
    row_spec = lambda w: pl.BlockSpec((tm, w), lambda i: (i, 0))
    return pl.pallas_call(
        functools.partial(_ffn_kernel, tm=tm, tps=tps, final=final),
        grid=(n // tm,),
        in_specs=[row_spec(d), row_spec(a.shape[1]), row_spec(b.shape[1])] + [_const_spec(w.shape) for w in weights],
        out_specs=[row_spec(d), pl.BlockSpec((1, CONV_W - 1, f), lambda i: (i // tps, 0, 0))],
        out_shape=[jax.ShapeDtypeStruct((n, d), F32), jax.ShapeDtypeStruct((n // seq_len, CONV_W - 1, f), F32)],
        scratch_shapes=[pltpu.VMEM((CONV_CARRY + tm, f), F32)],
        compiler_params=_params(1),
        name="outproj_ffn",
    )(x, a, b, *weights)


def _outproj_ffn_dec(x, a, b, weights, prev, final):
    n, d = x.shape
    f = weights[2].shape[1]
    return pl.pallas_call(
        functools.partial(_ffn_dec_kernel, final=final),
        out_shape=[jax.ShapeDtypeStruct((n, d), F32), jax.ShapeDtypeStruct((n, f), F32)],
        compiler_params=pltpu.CompilerParams(vmem_limit_bytes=VMEM_LIMIT),
        name="outproj_ffn_dec",
    )(x, a, b, *weights, prev[:, 0], prev[:, 1])


def _inproj1_kernel(x_ref, g_ref, w_ref, pw_ref, ps_ref, q_ref, kv_ref, d_ref, pool_ref, us_ref, *, tm, tps, width):
    step = pl.program_id(0) % tps

    @pl.when(step == 0)
    def _():
        us_ref[0:POOL_CARRY, :] = jnp.zeros((POOL_CARRY, width), F32)

    h = _rms(x_ref[...], g_ref[...]).astype(BF16)
    z = _dot(h, w_ref[...])
    q_ref[...] = z[:, 0:width]
    kv_ref[...] = z[:, width:3 * width]
    u = z[:, 3 * width:4 * width]
    us_ref[POOL_CARRY:POOL_CARRY + tm, :] = u
    pos = step * tm + lax.broadcasted_iota(jnp.int32, (tm, 1), 0)
    for g, win in enumerate(POOL_WINDOWS):
        sl = slice(g * LANES, (g + 1) * LANES)
        s = u[:, sl]
        for k in range(1, win):
            s = s + us_ref[POOL_CARRY - k:POOL_CARRY - k + tm, sl]
        cnt = jnp.minimum(pos + 1, win).astype(F32)
        pooled = (s / cnt - u[:, sl]).astype(BF16)
        d_ref[:, sl] = (_dot(pooled, pw_ref[g]) * ps_ref[:, sl]).astype(BF16)
    n_state = pool_ref.shape[1]
    pool_ref[0] = us_ref[POOL_CARRY + tm - n_state:POOL_CARRY + tm, :]
    us_ref[0:POOL_CARRY, :] = us_ref[tm:tm + POOL_CARRY, :]


def _inproj1(x, norm_g, w_in, pool_w, pool_scale, seq_len, tm):
    n, d = x.shape
    width = w_in.shape[1] // 4
    tps = seq_len // tm
    n_state = max(POOL_WINDOWS) - 1
    row_spec = lambda w: pl.BlockSpec((tm, w), lambda i: (i, 0))
    return pl.pallas_call(
        functools.partial(_inproj1_kernel, tm=tm, tps=tps, width=width),
        grid=(n // tm,),
        in_specs=[row_spec(d), _const_spec((1, d)), _const_spec(w_in.shape), _const_spec(pool_w.shape),
                  _const_spec((1, width))],
        out_specs=[row_spec(width), row_spec(2 * width), row_spec(width),
                   pl.BlockSpec((1, n_state, width), lambda i: (i // tps, 0, 0))],
        out_shape=[jax.ShapeDtypeStruct((n, width), F32), jax.ShapeDtypeStruct((n, 2 * width), F32),
                   jax.ShapeDtypeStruct((n, width), BF16), jax.ShapeDtypeStruct((n // seq_len, n_state, width), F32)],
        scratch_shapes=[pltpu.VMEM((POOL_CARRY + tm, width), F32)],
        compiler_params=_params(1),
        name="inproj1",
    )(x, norm_g.reshape(1, d), w_in, pool_w, pool_scale.reshape(1, width))


def _inproj1_dec_kernel(x_ref, g_ref, w_ref, pw_ref, ps_ref, st_ref, q_ref, kv_ref, d_ref, u_ref, *, width, pos):
    h = _rms(x_ref[...], g_ref[...]).astype(BF16)
    z = _dot(h, w_ref[...])
    q_ref[...] = z[:, 0:width]
    kv_ref[...] = z[:, width:3 * width]
    u = z[:, 3 * width:4 * width]
    u_ref[...] = u
    n_state = st_ref.shape[1] // width
    for g, win in enumerate(POOL_WINDOWS):
        sl = slice(g * LANES, (g + 1) * LANES)
        s = u[:, sl]
        for k in range(1, win):
            r = n_state - k
            s = s + st_ref[:, r * width + g * LANES:r * width + (g + 1) * LANES]
        pooled = (s / float(min(pos + 1, win)) - u[:, sl]).astype(BF16)
        d_ref[:, sl] = (_dot(pooled, pw_ref[g]) * ps_ref[:, sl]).astype(BF16)


def _inproj1_dec(x, norm_g, w_in, pool_w, pool_scale, state, pos):
    n, d = x.shape
    width = w_in.shape[1] // 4
    return pl.pallas_call(
        functools.partial(_inproj1_dec_kernel, width=width, pos=pos),
        out_shape=[jax.ShapeDtypeStruct((n, width), F32), jax.ShapeDtypeStruct((n, 2 * width), F32),
                   jax.ShapeDtypeStruct((n, width), BF16), jax.ShapeDtypeStruct((n, width), F32)],
        compiler_params=pltpu.CompilerParams(vmem_limit_bytes=VMEM_LIMIT),
        name="inproj1_dec",
    )(x, norm_g.reshape(1, d), w_in, pool_w, pool_scale.reshape(1, width), state.reshape(n, -1))


TM_DENSE = 256


def kernel(x_prompt, x_sample, cache_l0_kv, cache_l1_kv, page_table, state_l1_pool, state_ffn_conv, l0_norm, l0_w_in, l0_sgu_gain, l0_sgu_w, l0_sgu_b, l0_w_out, l1_norm, l1_w_in, l1_pool_w, l1_pool_scale, l1_w_out, ffn_norm, ffn_w_gate, ffn_w_up, ffn_conv_w, ffn_conv_b, ffn_w_down, final_norm):
    batch, seq_len, d = x_prompt.shape
    n_dec, dec_seq, _ = x_sample.shape
    page = cache_l0_kv.shape[1]
    past_len = page_table.shape[1] * page
    width = l0_w_out.shape[0] // 2
    n_heads = width // HEAD_DIM
    assert dec_seq == 1 and seq_len % TM_DENSE == 0 and past_len % MOBA_BLOCK == 0 and MOBA_BLOCK % page == 0
    assert past_len // MOBA_BLOCK > MOBA_TOPK

    bf = lambda w: w.astype(BF16)
    w_in0, w_in1, pool_w = bf(l0_w_in), bf(l1_w_in), bf(l1_pool_w)
    ffn_w = [_ffn_weights(bf(w_out), ffn_norm[i], bf(ffn_w_gate[i]), bf(ffn_w_up[i]), ffn_conv_w[i], ffn_conv_b[i],
                          bf(ffn_w_down[i]), final_norm)
             for i, w_out in enumerate((l0_w_out, l1_w_out))]

    xp = x_prompt.reshape(batch * seq_len, d)
    xs = x_sample.reshape(n_dec, d)

    tabs_p = _rope_tables(jnp.arange(seq_len, dtype=jnp.int32))
    q0, kv0_p, b0, km = _inproj0(xp, l0_norm, w_in0, tabs_p, l0_sgu_gain, l0_sgu_w, l0_sgu_b, seq_len, TM_DENSE)
    a0 = _moba_prefill(q0, kv0_p, km, batch, seq_len)
    x1p, conv0_p = _outproj_ffn(xp, a0, b0, ffn_w[0], seq_len, TM_DENSE, False)
    q1, kv1_p, d1, pool_p = _inproj1(x1p, l1_norm, w_in1, pool_w, l1_pool_scale, seq_len, TM_DENSE)
    c1 = _sb_prefill(q1, kv1_p, batch, seq_len)
    y_p, conv1_p = _outproj_ffn(x1p, c1, d1, ffn_w[1], seq_len, TM_DENSE, True)

    tabs_s = _rope_tables(jnp.full((1,), past_len, jnp.int32))
    q0s, kv0_s, b0s, sgu_v_s = _inproj0_dec(xs, l0_norm, w_in0, tabs_s, l0_sgu_gain, l0_sgu_w, l0_sgu_b)
    a0s = _paged_decode(_moba_dec_kernel, "moba_decode", cache_l0_kv, page_table, q0s, kv0_s)
    x1s, g0s = _outproj_ffn_dec(xs, a0s, b0s, ffn_w[0], state_ffn_conv[0], False)
    q1s, kv1_s, d1s, u_s = _inproj1_dec(x1s, l1_norm, w_in1, pool_w, l1_pool_scale, state_l1_pool, past_len)
    c1s = _paged_decode(_sb_dec_kernel, "sb_decode", cache_l1_kv, page_table, q1s, None)
    y_s, g1s = _outproj_ffn_dec(x1s, c1s, d1s, ffn_w[1], state_ffn_conv[1], True)

    kv_shape = lambda lead: lead + (2, n_heads, HEAD_DIM)
    conv_s = jnp.stack([jnp.stack([state_ffn_conv[i][:, 1], g], axis=1) for i, g in enumerate((g0s, g1s))], axis=0)
    pool_s = jnp.concatenate([state_l1_pool[:, 1:], u_s[:, None, :]], axis=1)
    return (y_p.reshape(batch, seq_len, d), y_s.reshape(n_dec, 1, d),
            kv0_p.reshape(kv_shape((batch, seq_len))), kv0_s.reshape(kv_shape((n_dec, 1))),
            sgu_v_s.reshape(n_dec, 1, width),
            kv1_p.reshape(kv_shape((batch, seq_len))), kv1_s.reshape(kv_shape((n_dec, 1))),
            pool_p, pool_s,
            jnp.stack([conv0_p, conv1_p], axis=0), conv_s)
```
